```python
import math
import jax, jax.numpy as jnp
from jax import lax
import numpy as np

D_MODEL = 1024
BATCH = 4
SEQ = 8192
DEPTH = 1
DEC_BATCH = 16
DEC_SEQ = 2048
PAST_LEN = 128

HEAD_DIM = 64
A_HEADS = 8
A_WIDTH = A_HEADS * HEAD_DIM
B_HEADS = 4
B_QK_WIDTH = 2 * B_HEADS * HEAD_DIM
B_WIDTH = B_HEADS * 2 * HEAD_DIM
D_MIX = A_WIDTH + B_WIDTH
IN_SPLITS = (A_WIDTH, A_WIDTH, A_WIDTH, A_WIDTH, B_QK_WIDTH, B_QK_WIDTH, B_WIDTH, B_WIDTH)
D_IN_PROJ = A_WIDTH * 4 + B_QK_WIDTH * 2 + B_WIDTH * 2
DILATED_PATTERNS = ((128, 1), (512, 4), (2048, 16))
BAND_BLOCK = 64
DENSE_Q_BLOCK = 128
ROPE_THETA = 500000.0
ROPE_DIM = HEAD_DIM // 4
NORM_EPS = 1e-6
MASK_VALUE = -1e30

kernel_name = "hybrid_dilated_diff_encoder"


def rmsnorm(x, g):
    xf = x.astype(jnp.float32)
    var = jnp.mean(xf * xf, axis=-1, keepdims=True)
    return (xf * lax.rsqrt(var + NORM_EPS) * g.astype(jnp.float32)).astype(x.dtype)


def rope_partial(x, pos):
    half = ROPE_DIM // 2
    inv = ROPE_THETA ** (-jnp.arange(half, dtype=jnp.float32) / half)
    ang = pos.astype(jnp.float32)[:, None] * inv[None, :]
    cos = jnp.cos(ang)[:, None, :]
    sin = jnp.sin(ang)[:, None, :]
    xr = x[..., :ROPE_DIM].astype(jnp.float32)
    x1, x2 = xr[..., :half], xr[..., half:]
    rot = jnp.concatenate([x1 * cos - x2 * sin, x2 * cos + x1 * sin], axis=-1)
    return jnp.concatenate([rot.astype(x.dtype), x[..., ROPE_DIM:]], axis=-1)


def dilated_window_attention(q, k, v, window, dil):
    bn, s, h, dh = q.shape
    half = window // (2 * dil)
    L = s // dil
    nb = -(-L // BAND_BLOCK)
    lp = nb * BAND_BLOCK
    kb_len = BAND_BLOCK + 2 * half

    def split(t):
        return t.reshape(bn, L, dil, h, dh).transpose(0, 2, 1, 3, 4)

    qs = jnp.pad(split(q), ((0, 0), (0, 0), (0, lp - L), (0, 0), (0, 0)))
    qs = qs.reshape(bn, dil, nb, BAND_BLOCK, h, dh)
    kpad = ((0, 0), (0, 0), (half, lp - L + half), (0, 0), (0, 0))
    idx = jnp.arange(nb)[:, None] * BAND_BLOCK + jnp.arange(kb_len)[None, :]
    ks = jnp.take(jnp.pad(split(k), kpad), idx, axis=2)
    vs = jnp.take(jnp.pad(split(v), kpad), idx, axis=2)

    sc = jnp.einsum('brnqhd,brnkhd->brnhqk', qs, ks).astype(jnp.float32) * (dh ** -0.5)
    qi = jnp.arange(nb)[:, None] * BAND_BLOCK + jnp.arange(BAND_BLOCK)[None, :]
    ki = idx - half
    rel = ki[:, None, :] - qi[:, :, None]
    valid = (jnp.abs(rel) <= half) & (ki[:, None, :] >= 0) & (ki[:, None, :] < L)
    sc = jnp.where(valid[:, None], sc, MASK_VALUE)
    lse = jax.nn.logsumexp(sc, axis=-1)
    p = jnp.exp(sc - lse[..., None])
    o = jnp.einsum('brnhqk,brnkhd->brnqhd', p.astype(v.dtype), vs)
    o = o.reshape(bn, dil, lp, h, dh)[:, :, :L].transpose(0, 2, 1, 3, 4).reshape(bn, s, h, dh)
    lse = lse.transpose(0, 1, 2, 4, 3).reshape(bn, dil, lp, h)[:, :, :L]
    lse = lse.transpose(0, 2, 1, 3).reshape(bn, s, h)
    return o, lse


def dilated_mixture(q, k, v):
    outs, lses = [], []
    for window, dil in DILATED_PATTERNS:
        o, lse = dilated_window_attention(q, k, v, window, dil)
        outs.append(o)
        lses.append(lse)
    w = jax.nn.softmax(jnp.stack(lses, axis=0), axis=0)
    o = jnp.sum(w[..., None] * jnp.stack(outs, axis=0).astype(jnp.float32), axis=0)
    return o.astype(q.dtype)


def diff_attention(q, k, v, lam, g_sub, lam_init):
    bn, s, h, _, dh = q.shape
    nq = s // DENSE_Q_BLOCK
    qb = q.reshape(bn, nq, DENSE_Q_BLOCK, h, 2, dh).transpose(1, 0, 2, 3, 4, 5)
    scale = dh ** -0.5

    def block(qblk):
        sc = jnp.einsum('bqhmd,bkhmd->bhmqk', qblk, k).astype(jnp.float32) * scale
        p = jax.nn.softmax(sc, axis=-1)
        a = p[:, :, 0] - lam * p[:, :, 1]
        return jnp.einsum('bhqk,bkhe->bqhe', a.astype(v.dtype), v)

    o = lax.map(block, qb)
    o = o.transpose(1, 0, 2, 3, 4).reshape(bn, s, h, 2 * dh)
    return rmsnorm(o, g_sub) * (1.0 - lam_init)


def encoder_layer(x, c, w_in, w_out, g_pre, g_post, w_ada, b_ada,
                  lam_q1, lam_k1, lam_q2, lam_k2, g_sub, lam_init):
    bn, s, _ = x.shape
    pos = jnp.arange(s)
    mod = jax.nn.silu(c) @ w_ada + b_ada
    shift, scale, gate = jnp.split(mod, 3, axis=-1)
    h = rmsnorm(x, g_pre) * (1.0 + scale[:, None, :]) + shift[:, None, :]
    proj = h @ w_in
    parts, off = [], 0
    for wdt in IN_SPLITS:
        parts.append(proj[..., off:off + wdt])
        off += wdt
    qa, ka, va, ga, qb, kb, vb, gb = parts

    qa = rope_partial(qa.reshape(bn, s, A_HEADS, HEAD_DIM), pos)
    ka = rope_partial(ka.reshape(bn, s, A_HEADS, HEAD_DIM), pos)
    va = va.reshape(bn, s, A_HEADS, HEAD_DIM)
    ya = dilated_mixture(qa, ka, va).reshape(bn, s, A_WIDTH) * jax.nn.silu(ga)

    qb = rope_partial(qb.reshape(bn, s, 2 * B_HEADS, HEAD_DIM), pos).reshape(bn, s, B_HEADS, 2, HEAD_DIM)
    kb = rope_partial(kb.reshape(bn, s, 2 * B_HEADS, HEAD_DIM), pos).reshape(bn, s, B_HEADS, 2, HEAD_DIM)
    vb = vb.reshape(bn, s, B_HEADS, 2 * HEAD_DIM)
    lam = (jnp.exp(jnp.sum(lam_q1.astype(jnp.float32) * lam_k1.astype(jnp.float32)))
           - jnp.exp(jnp.sum(lam_q2.astype(jnp.float32) * lam_k2.astype(jnp.float32))) + lam_init)
    yb = diff_attention(qb, kb, vb, lam, g_sub, lam_init).reshape(bn, s, B_WIDTH) * jax.nn.silu(gb)

    y = jnp.concatenate([ya, yb], axis=-1) @ w_out
    return x + gate[:, None, :] * rmsnorm(y, g_post)


def setup_inputs(seed: int = 0) -> dict:
    key = jax.random.key(seed)
    ks = jax.random.split(key, 16)
    f32 = jnp.float32
    return {
        "x_prompt": jax.random.normal(ks[0], (BATCH, SEQ, D_MODEL), f32),
        "x_sample": jax.random.normal(ks[1], (DEC_BATCH, DEC_SEQ, D_MODEL), f32),
        "c_prompt": jax.random.normal(ks[2], (BATCH, D_MODEL), f32),
        "c_sample": jax.random.normal(ks[3], (DEC_BATCH, D_MODEL), f32),
        "w_in": jax.random.normal(ks[4], (DEPTH, D_MODEL, D_IN_PROJ), f32) * D_MODEL ** -0.5,
        "w_out": jax.random.normal(ks[5], (DEPTH, D_MIX, D_MODEL), f32) * D_MIX ** -0.5,
        "g_pre": 1.0 + 0.02 * jax.random.normal(ks[6], (DEPTH, D_MODEL), f32),
        "g_post": 1.0 + 0.02 * jax.random.normal(ks[7], (DEPTH, D_MODEL), f32),
        "w_ada": jax.random.normal(ks[8], (DEPTH, D_MODEL, 3 * D_MODEL), f32) * (0.5 * D_MODEL ** -0.5),
        "b_ada": 0.01 * jax.random.normal(ks[9], (DEPTH, 3 * D_MODEL), f32),
        "lam_q1": 0.1 * jax.random.normal(ks[10], (DEPTH, HEAD_DIM), f32),
        "lam_k1": 0.1 * jax.random.normal(ks[11], (DEPTH, HEAD_DIM), f32),
        "lam_q2": 0.1 * jax.random.normal(ks[12], (DEPTH, HEAD_DIM), f32),
        "lam_k2": 0.1 * jax.random.normal(ks[13], (DEPTH, HEAD_DIM), f32),
        "g_sub": 1.0 + 0.02 * jax.random.normal(ks[14], (DEPTH, 2 * HEAD_DIM), f32),
    }


def reference(x_prompt, x_sample, c_prompt, c_sample, w_in, w_out, g_pre, g_post,
              w_ada, b_ada, lam_q1, lam_k1, lam_q2, lam_k2, g_sub):
    y_prompt = x_prompt
    y_sample = x_sample
    for l in range(DEPTH):
        lam_init = 0.8 - 0.6 * math.exp(-0.3 * l)
        y_prompt = encoder_layer(y_prompt, c_prompt, w_in[l], w_out[l], g_pre[l], g_post[l],
                                 w_ada[l], b_ada[l], lam_q1[l], lam_k1[l], lam_q2[l], lam_k2[l],
                                 g_sub[l], lam_init)
        y_sample = encoder_layer(y_sample, c_sample, w_in[l], w_out[l], g_pre[l], g_post[l],
                                 w_ada[l], b_ada[l], lam_q1[l], lam_k1[l], lam_q2[l], lam_k2[l],
                                 g_sub[l], lam_init)
    return (y_prompt, y_sample)
```

```python
import functools
import math

import numpy as np
import jax
import jax.numpy as jnp
from jax import lax
from jax.experimental import pallas as pl
from jax.experimental.pallas import tpu as pltpu

F32 = jnp.float32
BF16 = jnp.bfloat16

D_MODEL = 1024
HEAD_DIM = 64
SEG = 512
LANES = 128
ROPE_DIM = HEAD_DIM // 4
ROPE_THETA = 500000.0
NORM_EPS = 1e-6
MASK_VALUE = -1e30
DILATED_PATTERNS = ((128, 1), (512, 4), (2048, 16))
A_BLOCK = 256
A_REACH = max(w // 2 for w, _ in DILATED_PATTERNS)
A_NDELTA = 2 * (A_REACH // A_BLOCK) + 1
VMEM_LIMIT = 56 * 1024 * 1024

NT_DIMS = (((1,), (1,)), ((), ()))


def _silu(x):
    return x / (1.0 + jnp.exp(-x))


def _ada_kernel(c_ref, w_ref, b_ref, q1_ref, k1_ref, q2_ref, k2_ref, mod_ref, lam_ref, *, lam_init):
    a = _silu(c_ref[...])
    mod_ref[...] = jnp.dot(a, w_ref[...], precision=lax.Precision.HIGHEST,
                           preferred_element_type=F32) + b_ref[...]
    lam = (jnp.exp(jnp.sum(q1_ref[...] * k1_ref[...], keepdims=True))
           - jnp.exp(jnp.sum(q2_ref[...] * k2_ref[...], keepdims=True)) + lam_init)
    lam_ref[...] = jnp.broadcast_to(lam, lam_ref.shape)


def _modulation(c_all, w_ada, b_ada, lq1, lk1, lq2, lk2, lam_init):
    nb = c_all.shape[0]
    small = pl.BlockSpec((1, HEAD_DIM), lambda j: (0, 0))
    return pl.pallas_call(
        functools.partial(_ada_kernel, lam_init=lam_init),
        grid=(3,),
        in_specs=[pl.BlockSpec((nb, D_MODEL), lambda j: (0, 0)),
                  pl.BlockSpec((D_MODEL, D_MODEL), lambda j: (0, j)),
                  pl.BlockSpec((1, D_MODEL), lambda j: (0, j)),
                  small, small, small, small],
        out_specs=[pl.BlockSpec((nb, D_MODEL), lambda j: (0, j)),
                   pl.BlockSpec((8, LANES), lambda j: (0, 0))],
        out_shape=[jax.ShapeDtypeStruct((nb, 3 * D_MODEL), F32),
                   jax.ShapeDtypeStruct((8, LANES), F32)],
        compiler_params=pltpu.CompilerParams(dimension_semantics=("arbitrary",),
                                             vmem_limit_bytes=VMEM_LIMIT),
        name="ada_modulation",
    )(c_all, w_ada, b_ada.reshape(1, -1), lq1.reshape(1, -1), lk1.reshape(1, -1),
      lq2.reshape(1, -1), lk2.reshape(1, -1))


def _rope_tables(seq):
    half = ROPE_DIM // 2
    inv = ROPE_THETA ** (-jnp.arange(half, dtype=F32) / half)
    ang = jnp.arange(seq).astype(F32)[:, None] * inv[None, :]
    cos, sin = jnp.cos(ang), jnp.sin(ang)
    ones = jnp.ones((seq, HEAD_DIM - ROPE_DIM), F32)
    zeros = jnp.zeros((seq, HEAD_DIM - ROPE_DIM), F32)
    zh = jnp.zeros((seq, half), F32)
    c = jnp.concatenate([cos, cos, ones], axis=1)
    sa = jnp.concatenate([-sin, zh, zeros], axis=1)
    sb = jnp.concatenate([zh, sin, zeros], axis=1)
    rep = LANES // HEAD_DIM
    return jnp.tile(c, (1, rep)), jnp.tile(sa, (1, rep)), jnp.tile(sb, (1, rep))


def _inproj_kernel(x_ref, mod_ref, gpre_ref, wm_ref, wvt_ref, cos_ref, sa_ref, sb_ref,
                   qa_ref, ka_ref, ga_ref, qb_ref, kb_ref, gb_ref, vat_ref, vbt_ref):
    x = x_ref[0]
    var = jnp.mean(x * x, axis=-1, keepdims=True)
    xn = x * lax.rsqrt(var + NORM_EPS) * gpre_ref[...]
    mod = mod_ref[0]
    shift = mod[:, :D_MODEL]
    scale = mod[:, D_MODEL:2 * D_MODEL]
    h = (xn * (1.0 + scale) + shift).astype(BF16)

    cos, sa, sb = cos_ref[...], sa_ref[...], sb_ref[...]
    half = ROPE_DIM // 2

    def rope(seg, mult):
        cols = []
        for c in range(SEG // LANES):
            xc = seg[:, c * LANES:(c + 1) * LANES]
            r = (xc * cos + pltpu.roll(xc, LANES - half, axis=1) * sa
                 + pltpu.roll(xc, half, axis=1) * sb)
            cols.append(r * mult if mult != 1.0 else r)
        return jnp.concatenate(cols, axis=1)

    qk_scale = HEAD_DIM ** -0.5
    plan = ((qa_ref, True, qk_scale), (ka_ref, True, 1.0), (ga_ref, False, 1.0),
            (qb_ref, True, qk_scale), (kb_ref, True, 1.0), (gb_ref, False, 1.0))
    for s, (ref, roped, mult) in enumerate(plan):
        seg = jnp.dot(h, wm_ref[:, s * SEG:(s + 1) * SEG], preferred_element_type=F32)
        if roped:
            seg = rope(seg, mult)
        ref[0] = seg.astype(ref.dtype)

    vt = lax.dot_general(wvt_ref[...], h, NT_DIMS, preferred_element_type=F32)
    vat_ref[0] = vt[:SEG].astype(vat_ref.dtype)
    vbt_ref[0] = vt[SEG:].astype(vbt_ref.dtype)


def _inproj(x, mod3, g_pre, wm, wvt, tables, tm):
    bn, s, d = x.shape
    cos, sa, sb = tables
    tok = lambda: pl.BlockSpec((1, tm, SEG), lambda b, i: (b, i, 0))
    tokt = lambda: pl.BlockSpec((1, SEG, tm), lambda b, i: (b, 0, i))
    tab = lambda: pl.BlockSpec((tm, LANES), lambda b, i: (i, 0))
    nat = jax.ShapeDtypeStruct((bn, s, SEG), BF16)
    tr = jax.ShapeDtypeStruct((bn, SEG, s), BF16)
    return pl.pallas_call(
        _inproj_kernel,
        grid=(bn, s // tm),
        in_specs=[pl.BlockSpec((1, tm, d), lambda b, i: (b, i, 0)),
                  pl.BlockSpec((1, 1, 3 * d), lambda b, i: (b, 0, 0)),
                  pl.BlockSpec((1, d), lambda b, i: (0, 0)),
                  pl.BlockSpec(wm.shape, lambda b, i: (0, 0)),
                  pl.BlockSpec(wvt.shape, lambda b, i: (0, 0)),
                  tab(), tab(), tab()],
        out_specs=[tok(), tok(), tok(), tok(), tok(), tok(), tokt(), tokt()],
        out_shape=[nat, nat, nat, nat, nat, nat, tr, tr],
        compiler_params=pltpu.CompilerParams(dimension_semantics=("arbitrary", "arbitrary"),
                                             vmem_limit_bytes=VMEM_LIMIT),
        name="inproj_rope",
    )(x, mod3, g_pre.reshape(1, d), wm, wvt, cos[:s], sa[:s], sb[:s])


def _band_bias():
    nd = A_REACH // A_BLOCK
    delta = np.arange(-nd, nd + 1)[:, None, None]
    key = np.arange(A_BLOCK)[None, :, None]
    qry = np.arange(A_BLOCK)[None, None, :]
    rel = delta * A_BLOCK + key - qry
    mult = np.zeros(rel.shape, np.int64)
    for window, dil in DILATED_PATTERNS:
        mult += ((rel % dil == 0) & (np.abs(rel) <= window // 2)).astype(np.int64)
    bias = np.where(mult > 0, np.log(np.maximum(mult, 1)), MASK_VALUE)
    return jnp.asarray(bias, F32)


def _attn_kernel(*refs, banded, tq, tk, nkb, post_scale):
    if banded:
        q_ref, k_ref, vt_ref, g_ref, bias_ref, o_ref, q2_scr = refs
    else:
        q_ref, k_ref, vt_ref, g_ref, gsub_ref, lam_ref, o_ref, q2_scr = refs
    i = pl.program_id(2)

    q = q_ref[0].astype(F32)
    lane = lax.broadcasted_iota(jnp.int32, q.shape, 1)
    q2_scr[0] = jnp.where(lane < HEAD_DIM, q, 0.0).astype(BF16)
    q2_scr[1] = jnp.where(lane >= HEAD_DIM, q, 0.0).astype(BF16)

    vrows = HEAD_DIM if banded else LANES

    def update(s, m, l, acc, v):
        m_new = jnp.maximum(m, jnp.max(s, axis=0, keepdims=True))
        alpha = jnp.exp(m - m_new)
        p = jnp.exp(s - m_new)
        l = alpha * l + jnp.sum(p, axis=0, keepdims=True)
        acc = alpha * acc + jnp.dot(v, p.astype(BF16), preferred_element_type=F32)
        return m_new, l, acc

    def body(j, carry):
        m0, l0, a0, m1, l1, a1 = carry
        off = pl.multiple_of(j * tk, tk)
        kk = k_ref[0, pl.ds(off, tk), :]
        vt = vt_ref[0, :, pl.ds(off, tk)]
        s0 = lax.dot_general(kk, q2_scr[0], NT_DIMS, preferred_element_type=F32)
        s1 = lax.dot_general(kk, q2_scr[1], NT_DIMS, preferred_element_type=F32)
        if banded:
            b = bias_ref[j - i + A_NDELTA // 2]
            s0 = s0 + b
            s1 = s1 + b
            v0, v1 = vt[:HEAD_DIM], vt[HEAD_DIM:]
        else:
            v0 = v1 = vt
        m0, l0, a0 = update(s0, m0, l0, a0, v0)
        m1, l1, a1 = update(s1, m1, l1, a1, v1)
        return m0, l0, a0, m1, l1, a1

    mi = jnp.full((1, tq), MASK_VALUE, F32)
    li = jnp.zeros((1, tq), F32)
    ai = jnp.zeros((vrows, tq), F32)
    if banded:
        nd = A_NDELTA // 2
        lo, hi = jnp.maximum(i - nd, 0), jnp.minimum(i + nd, nkb - 1) + 1
    else:
        lo, hi = 0, nkb
    _, l0, a0, _, l1, a1 = lax.fori_loop(lo, hi, body, (mi, li, ai, mi, li, ai))

    if banded:
        o_t = jnp.concatenate([a0 / l0, a1 / l1], axis=0)
    else:
        d = a0 / l0 - lam_ref[0, 0] * (a1 / l1)
        var = jnp.mean(d * d, axis=0, keepdims=True)
        o_t = d * lax.rsqrt(var + NORM_EPS) * gsub_ref[...] * post_scale
    g = g_ref[0].astype(F32)
    o_ref[0] = (o_t.T * _silu(g)).astype(o_ref.dtype)


def _attention(q, k, vt, g, *, banded, tq, tk, extra, post_scale=1.0):
    bn, s, _ = q.shape
    nh = SEG // LANES
    kernel = functools.partial(_attn_kernel, banded=banded, tq=tq, tk=tk, nkb=s // tk,
                               post_scale=post_scale)
    qspec = lambda: pl.BlockSpec((1, tq, LANES), lambda b, h, i: (b, i, h))
    in_specs = [qspec(),
                pl.BlockSpec((1, s, LANES), lambda b, h, i: (b, 0, h)),
                pl.BlockSpec((1, LANES, s), lambda b, h, i: (b, h, 0)),
                qspec()]
    if banded:
        (bias,) = extra
        in_specs.append(pl.BlockSpec(bias.shape, lambda b, h, i: (0, 0, 0)))
    else:
        in_specs.append(pl.BlockSpec((LANES, 1), lambda b, h, i: (0, 0)))
        in_specs.append(pl.BlockSpec(memory_space=pltpu.SMEM))
    return pl.pallas_call(
        kernel,
        grid=(bn, nh, s // tq),
        in_specs=in_specs,
        out_specs=qspec(),
        out_shape=jax.ShapeDtypeStruct((bn, s, SEG), BF16),
        scratch_shapes=[pltpu.VMEM((2, tq, LANES), BF16)],
        compiler_params=pltpu.CompilerParams(
            dimension_semantics=("arbitrary", "arbitrary", "arbitrary"),
            vmem_limit_bytes=VMEM_LIMIT),
        name="mixer_a_banded" if banded else "mixer_b_diff",
    )(q, k, vt, g, *extra)


def _outproj_kernel(ya_ref, yb_ref, w_ref, x_ref, mod_ref, gpost_ref, o_ref):
    y = (jnp.dot(ya_ref[0], w_ref[:SEG, :], preferred_element_type=F32)
         + jnp.dot(yb_ref[0], w_ref[SEG:, :], preferred_element_type=F32))
    var = jnp.mean(y * y, axis=-1, keepdims=True)
    yn = y * lax.rsqrt(var + NORM_EPS) * gpost_ref[...]
    gate = mod_ref[0][:, 2 * D_MODEL:]
    o_ref[0] = x_ref[0] + gate * yn


def _outproj(ya, yb, w_out, x, mod3, g_post, tm):
    bn, s, d = x.shape
    tok = lambda: pl.BlockSpec((1, tm, SEG), lambda b, i: (b, i, 0))
    full = lambda: pl.BlockSpec((1, tm, d), lambda b, i: (b, i, 0))
    return pl.pallas_call(
        _outproj_kernel,
        grid=(bn, s // tm),
        in_specs=[tok(), tok(),
                  pl.BlockSpec(w_out.shape, lambda b, i: (0, 0)),
                  full(),
                  pl.BlockSpec((1, 1, 3 * d), lambda b, i: (b, 0, 0)),
                  pl.BlockSpec((1, d), lambda b, i: (0, 0))],
        out_specs=full(),
        out_shape=jax.ShapeDtypeStruct((bn, s, d), F32),
        compiler_params=pltpu.CompilerParams(dimension_semantics=("arbitrary", "arbitrary"),
                                             vmem_limit_bytes=VMEM_LIMIT),
        name="outproj_residual",
    )(ya, yb, w_out, x, mod3, g_post.reshape(1, d))


def _group_layer(x, mod3, lam, wm, wvt, w_out_bf, g_pre, g_post, g_sub, tables, bias, lam_init):
    s = x.shape[1]
    assert s % A_BLOCK == 0 and s % 512 == 0
    qa, ka, ga, qb, kb, gb, vat, vbt = _inproj(x, mod3, g_pre, wm, wvt, tables, tm=512)
    ya = _attention(qa, ka, vat, ga, banded=True, tq=A_BLOCK, tk=A_BLOCK, extra=(bias,))
    yb = _attention(qb, kb, vbt, gb, banded=False, tq=256, tk=512,
                    extra=(g_sub.reshape(LANES, 1), lam), post_scale=1.0 - lam_init)
    return _outproj(ya, yb, w_out_bf, x, mod3, g_post, tm=512)


def kernel(x_prompt, x_sample, c_prompt, c_sample, w_in, w_out, g_pre, g_post, w_ada, b_ada,
           lam_q1, lam_k1, lam_q2, lam_k2, g_sub):
    depth = w_in.shape[0]
    nbp = x_prompt.shape[0]
    tables = _rope_tables(max(x_prompt.shape[1], x_sample.shape[1]))
    bias = _band_bias()
    y_prompt, y_sample = x_prompt, x_sample
    for l in range(depth):
        lam_init = 0.8 - 0.6 * math.exp(-0.3 * l)
        wl = w_in[l]
        seg = lambda n: wl[:, n * SEG:(n + 1) * SEG]
        wm = jnp.concatenate([seg(0), seg(1), seg(3), seg(4), seg(5), seg(7)], axis=1).astype(BF16)
        wvt = jnp.concatenate([seg(2), seg(6)], axis=1).T.astype(BF16)
        w_out_bf = w_out[l].astype(BF16)
        c_all = jnp.concatenate([c_prompt, c_sample], axis=0)
        mod, lam_tile = _modulation(c_all, w_ada[l], b_ada[l], lam_q1[l], lam_k1[l],
                                    lam_q2[l], lam_k2[l], lam_init)
        lam = lam_tile[:1, :1]
        mod3 = mod[:, None, :]
        args = (wm, wvt, w_out_bf, g_pre[l], g_post[l], g_sub[l], tables, bias, lam_init)
        y_prompt = _group_layer(y_prompt, mod3[:nbp], lam, *args)
        y_sample = _group_layer(y_sample, mod3[nbp:], lam, *args)
    return (y_prompt, y_sample)
```

```python
import functools
import math

import numpy as np
import jax
import jax.numpy as jnp
from jax import lax
from jax.experimental import pallas as pl
from jax.experimental.pallas import tpu as pltpu

F32 = jnp.float32
BF16 = jnp.bfloat16

D_MODEL = 1024
HEAD_DIM = 64
SEG = 512
LANES = 128
ROPE_DIM = HEAD_DIM // 4
ROPE_THETA = 500000.0
NORM_EPS = 1e-6
MASK_VALUE = -1e30
DILATED_PATTERNS = ((128, 1), (512, 4), (2048, 16))
A_REACH = max(w // 2 for w, _ in DILATED_PATTERNS)
A_TQ, A_TK = 512, 256
B_TQ, B_TK = 256, 512
PROJ_TM = 512
VMEM_LIMIT = 56 * 1024 * 1024

NT_DIMS = (((1,), (1,)), ((), ()))


def _silu(x):
    return x / (1.0 + jnp.exp(-x))


def _ada_kernel(c_ref, w_ref, b_ref, q1_ref, k1_ref, q2_ref, k2_ref, mod_ref, lam_ref, *, lam_init):
    a = _silu(c_ref[...])
    mod_ref[...] = jnp.dot(a, w_ref[...], precision=lax.Precision.HIGHEST,
                           preferred_element_type=F32) + b_ref[...]
    lam = (jnp.exp(jnp.sum(q1_ref[...] * k1_ref[...], keepdims=True))
           - jnp.exp(jnp.sum(q2_ref[...] * k2_ref[...], keepdims=True)) + lam_init)
    lam_ref[...] = jnp.broadcast_to(lam, lam_ref.shape)


def _modulation(c_all, w_ada, b_ada, lq1, lk1, lq2, lk2, lam_init):
    nb = c_all.shape[0]
    small = pl.BlockSpec((1, HEAD_DIM), lambda j: (0, 0))
    return pl.pallas_call(
        functools.partial(_ada_kernel, lam_init=lam_init),
        grid=(3,),
        in_specs=[pl.BlockSpec((nb, D_MODEL), lambda j: (0, 0)),
                  pl.BlockSpec((D_MODEL, D_MODEL), lambda j: (0, j)),
                  pl.BlockSpec((1, D_MODEL), lambda j: (0, j)),
                  small, small, small, small],
        out_specs=[pl.BlockSpec((nb, D_MODEL), lambda j: (0, j)),
                   pl.BlockSpec((8, LANES), lambda j: (0, 0))],
        out_shape=[jax.ShapeDtypeStruct((nb, 3 * D_MODEL), F32),
                   jax.ShapeDtypeStruct((8, LANES), F32)],
        compiler_params=pltpu.CompilerParams(dimension_semantics=("arbitrary",),
                                             vmem_limit_bytes=VMEM_LIMIT),
        name="ada_modulation",
    )(c_all, w_ada, b_ada.reshape(1, -1), lq1.reshape(1, -1), lk1.reshape(1, -1),
      lq2.reshape(1, -1), lk2.reshape(1, -1))


def _rope_tables(seq):
    half = ROPE_DIM // 2
    inv = ROPE_THETA ** (-jnp.arange(half, dtype=F32) / half)
    ang = jnp.arange(seq).astype(F32)[:, None] * inv[None, :]
    cos, sin = jnp.cos(ang), jnp.sin(ang)
    ones = jnp.ones((seq, HEAD_DIM - ROPE_DIM), F32)
    zeros = jnp.zeros((seq, HEAD_DIM - ROPE_DIM), F32)
    zh = jnp.zeros((seq, half), F32)
    c = jnp.concatenate([cos, cos, ones], axis=1)
    sa = jnp.concatenate([-sin, zh, zeros], axis=1)
    sb = jnp.concatenate([zh, sin, zeros], axis=1)
    rep = LANES // HEAD_DIM
    return jnp.tile(c, (1, rep)), jnp.tile(sa, (1, rep)), jnp.tile(sb, (1, rep))


def _inproj_kernel(x_ref, mod_ref, gpre_ref, wm_ref, wvt_ref, cos_ref, sa_ref, sb_ref,
                   qa_ref, ka_ref, ga_ref, qb_ref, kb_ref, gb_ref, vat_ref, vbt_ref):
    x = x_ref[0]
    var = jnp.mean(x * x, axis=-1, keepdims=True)
    xn = x * lax.rsqrt(var + NORM_EPS) * gpre_ref[...]
    mod = mod_ref[0]
    shift = mod[:, :D_MODEL]
    scale = mod[:, D_MODEL:2 * D_MODEL]
    h = (xn * (1.0 + scale) + shift).astype(BF16)

    cos, sa, sb = cos_ref[...], sa_ref[...], sb_ref[...]
    half = ROPE_DIM // 2

    def rope(seg, mult):
        cols = []
        for c in range(SEG // LANES):
            xc = seg[:, c * LANES:(c + 1) * LANES]
            r = (xc * cos + pltpu.roll(xc, LANES - half, axis=1) * sa
                 + pltpu.roll(xc, half, axis=1) * sb)
            cols.append(r * mult if mult != 1.0 else r)
        return jnp.concatenate(cols, axis=1)

    qk_scale = HEAD_DIM ** -0.5 * math.log2(math.e)
    plan = ((qa_ref, True, qk_scale), (ka_ref, True, 1.0), (ga_ref, False, 1.0),
            (qb_ref, True, qk_scale), (kb_ref, True, 1.0), (gb_ref, False, 1.0))
    for s, (ref, roped, mult) in enumerate(plan):
        seg = jnp.dot(h, wm_ref[:, s * SEG:(s + 1) * SEG], preferred_element_type=F32)
        if roped:
            seg = rope(seg, mult)
        ref[0] = seg.astype(ref.dtype)

    vt = lax.dot_general(wvt_ref[...], h, NT_DIMS, preferred_element_type=F32)
    vat_ref[0] = vt[:SEG].astype(vat_ref.dtype)
    vbt_ref[0] = vt[SEG:].astype(vbt_ref.dtype)


def _inproj(x, mod3, g_pre, wm, wvt, tables, tm):
    bn, s, d = x.shape
    cos, sa, sb = tables
    tok = lambda: pl.BlockSpec((1, tm, SEG), lambda b, i: (b, i, 0))
    tokt = lambda: pl.BlockSpec((1, SEG, tm), lambda b, i: (b, 0, i))
    tab = lambda: pl.BlockSpec((tm, LANES), lambda b, i: (i, 0))
    nat = jax.ShapeDtypeStruct((bn, s, SEG), BF16)
    tr = jax.ShapeDtypeStruct((bn, SEG, s), BF16)
    return pl.pallas_call(
        _inproj_kernel,
        grid=(bn, s // tm),
        in_specs=[pl.BlockSpec((1, tm, d), lambda b, i: (b, i, 0)),
                  pl.BlockSpec((1, 1, 3 * d), lambda b, i: (b, 0, 0)),
                  pl.BlockSpec((1, d), lambda b, i: (0, 0)),
                  pl.BlockSpec(wm.shape, lambda b, i: (0, 0)),
                  pl.BlockSpec(wvt.shape, lambda b, i: (0, 0)),
                  tab(), tab(), tab()],
        out_specs=[tok(), tok(), tok(), tok(), tok(), tok(), tokt(), tokt()],
        out_shape=[nat, nat, nat, nat, nat, nat, tr, tr],
        compiler_params=pltpu.CompilerParams(dimension_semantics=("arbitrary", "arbitrary"),
                                             vmem_limit_bytes=VMEM_LIMIT),
        name="inproj_rope",
    )(x, mod3, g_pre.reshape(1, d), wm, wvt, cos[:s], sa[:s], sb[:s])


def _band_bias(tq, tk):
    nd = A_REACH // tk
    delta = np.arange(-nd, nd + tq // tk)[:, None, None]
    key = np.arange(tk)[None, :, None]
    qry = np.arange(tq)[None, None, :]
    rel = delta * tk + key - qry
    mult = np.zeros(rel.shape, np.int64)
    for window, dil in DILATED_PATTERNS:
        mult += ((rel % dil == 0) & (np.abs(rel) <= window // 2)).astype(np.int64)
    bias = np.where(mult > 0, np.log2(np.maximum(mult, 1)), MASK_VALUE)
    return jnp.asarray(bias, F32)


def _attn_kernel(*refs, banded, tq, tk, nkb, post_scale):
    if banded:
        q_ref, k_ref, vt_ref, g_ref, bias_ref, o_ref = refs[:6]
    else:
        q_ref, k_ref, vt_ref, g_ref, gsub_ref, lam_ref, o_ref = refs[:7]
    q2_scr, s_scr, cm_scr, p_scr, acc_scr = refs[-5:]
    i = pl.program_id(2)

    q = q_ref[0].astype(F32)
    lane = lax.broadcasted_iota(jnp.int32, q.shape, 1)
    q2_scr[0] = jnp.where(lane < HEAD_DIM, q, 0.0).astype(BF16)
    q2_scr[1] = jnp.where(lane >= HEAD_DIM, q, 0.0).astype(BF16)
    acc_scr[...] = jnp.zeros(acc_scr.shape, F32)

    if banded:
        ratio, nd = tq // tk, A_REACH // tk
        lo = jnp.maximum(i * ratio - nd, 0)
        npairs = (jnp.minimum(i * ratio + ratio + nd, nkb) - lo) // 2
    else:
        lo, npairs = 0, nkb // 2

    def qk_stage(j, slot):
        off = pl.multiple_of(j * tk, tk)
        kk = k_ref[0, pl.ds(off, tk), :]
        for mp in range(2):
            s = lax.dot_general(kk, q2_scr[mp], NT_DIMS, preferred_element_type=F32)
            if banded:
                s = s + bias_ref[j - i * ratio + nd]
            s_scr[slot, mp] = s
            cm_scr[slot, mp] = jnp.max(s, axis=0, keepdims=True)

    def sm_stage(slot, m, l):
        m_out, l_out, alpha_out = [], [], []
        for mp in range(2):
            m_new = jnp.maximum(m[mp], cm_scr[slot, mp])
            alpha = jnp.exp2(m[mp] - m_new)
            p = jnp.exp2(s_scr[slot, mp] - m_new)
            l_out.append(alpha * l[mp] + jnp.sum(p, axis=0, keepdims=True))
            p_scr[slot, mp] = p.astype(BF16)
            m_out.append(m_new)
            alpha_out.append(alpha)
        return tuple(m_out), tuple(l_out), tuple(alpha_out)

    def pv_stage(j, slot, alpha):
        off = pl.multiple_of(j * tk, tk)
        vt = vt_ref[0, :, pl.ds(off, tk)]
        for mp in range(2):
            v = vt[mp * HEAD_DIM:(mp + 1) * HEAD_DIM] if banded else vt
            acc_scr[mp] = alpha[mp] * acc_scr[mp] + jnp.dot(v, p_scr[slot, mp],
                                                            preferred_element_type=F32)

    m = (jnp.full((1, tq), MASK_VALUE, F32),) * 2
    l = (jnp.zeros((1, tq), F32),) * 2
    qk_stage(lo, 0)
    qk_stage(lo + 1, 1)
    m, l, alpha = sm_stage(0, m, l)

    def body(u, carry):
        m, l, alpha0 = carry
        r = lo + 2 * u
        qk_stage(r + 2, 0)
        m, l, alpha1 = sm_stage(1, m, l)
        pv_stage(r, 0, alpha0)
        qk_stage(r + 3, 1)
        m, l, alpha0 = sm_stage(0, m, l)
        pv_stage(r + 1, 1, alpha1)
        return m, l, alpha0

    m, l, alpha0 = lax.fori_loop(0, npairs - 1, body, (m, l, alpha))
    r = lo + 2 * (npairs - 1)
    m, l, alpha1 = sm_stage(1, m, l)
    pv_stage(r, 0, alpha0)
    pv_stage(r + 1, 1, alpha1)

    if banded:
        o_t = jnp.concatenate([acc_scr[0] / l[0], acc_scr[1] / l[1]], axis=0)
    else:
        d = acc_scr[0] / l[0] - lam_ref[0, 0] * (acc_scr[1] / l[1])
        var = jnp.mean(d * d, axis=0, keepdims=True)
        o_t = d * lax.rsqrt(var + NORM_EPS) * gsub_ref[...] * post_scale
    g = g_ref[0].astype(F32)
    o_ref[0] = (o_t.T * _silu(g)).astype(o_ref.dtype)


def _attention(q, k, vt, g, *, banded, tq, tk, extra, post_scale=1.0):
    bn, s, _ = q.shape
    nh = SEG // LANES
    nkb = s // tk
    assert s % tq == 0 and s % tk == 0 and nkb % 2 == 0
    if banded:
        assert tq % (2 * tk) == 0 and A_REACH % (2 * tk) == 0
    kernel = functools.partial(_attn_kernel, banded=banded, tq=tq, tk=tk, nkb=nkb,
                               post_scale=post_scale)
    qspec = lambda: pl.BlockSpec((1, tq, LANES), lambda b, h, i: (b, i, h))
    in_specs = [qspec(),
                pl.BlockSpec((1, s, LANES), lambda b, h, i: (b, 0, h)),
                pl.BlockSpec((1, LANES, s), lambda b, h, i: (b, h, 0)),
                qspec()]
    if banded:
        (bias,) = extra
        in_specs.append(pl.BlockSpec(bias.shape, lambda b, h, i: (0, 0, 0)))
    else:
        in_specs.append(pl.BlockSpec((LANES, 1), lambda b, h, i: (0, 0)))
        in_specs.append(pl.BlockSpec(memory_space=pltpu.SMEM))
    vrows = HEAD_DIM if banded else LANES
    return pl.pallas_call(
        kernel,
        grid=(bn, nh, s // tq),
        in_specs=in_specs,
        out_specs=qspec(),
        out_shape=jax.ShapeDtypeStruct((bn, s, SEG), BF16),
        scratch_shapes=[pltpu.VMEM((2, tq, LANES), BF16),
                        pltpu.VMEM((2, 2, tk, tq), F32),
                        pltpu.VMEM((2, 2, 1, tq), F32),
                        pltpu.VMEM((2, 2, tk, tq), BF16),
                        pltpu.VMEM((2, vrows, tq), F32)],
        compiler_params=pltpu.CompilerParams(
            dimension_semantics=("arbitrary", "arbitrary", "arbitrary"),
            vmem_limit_bytes=VMEM_LIMIT),
        name="mixer_a_banded" if banded else "mixer_b_diff",
    )(q, k, vt, g, *extra)


def _outproj_kernel(ya_ref, yb_ref, w_ref, x_ref, mod_ref, gpost_ref, o_ref):
    y = (jnp.dot(ya_ref[0], w_ref[:SEG, :], preferred_element_type=F32)
         + jnp.dot(yb_ref[0], w_ref[SEG:, :], preferred_element_type=F32))
    var = jnp.mean(y * y, axis=-1, keepdims=True)
    yn = y * lax.rsqrt(var + NORM_EPS) * gpost_ref[...]
    gate = mod_ref[0][:, 2 * D_MODEL:]
    o_ref[0] = x_ref[0] + gate * yn


def _outproj(ya, yb, w_out, x, mod3, g_post, tm):
    bn, s, d = x.shape
    tok = lambda: pl.BlockSpec((1, tm, SEG), lambda b, i: (b, i, 0))
    full = lambda: pl.BlockSpec((1, tm, d), lambda b, i: (b, i, 0))
    return pl.pallas_call(
        _outproj_kernel,
        grid=(bn, s // tm),
        in_specs=[tok(), tok(),
                  pl.BlockSpec(w_out.shape, lambda b, i: (0, 0)),
                  full(),
                  pl.BlockSpec((1, 1, 3 * d), lambda b, i: (b, 0, 0)),
                  pl.BlockSpec((1, d), lambda b, i: (0, 0))],
        out_specs=full(),
        out_shape=jax.ShapeDtypeStruct((bn, s, d), F32),
        compiler_params=pltpu.CompilerParams(dimension_semantics=("arbitrary", "arbitrary"),
                                             vmem_limit_bytes=VMEM_LIMIT),
        name="outproj_residual",
    )(ya, yb, w_out, x, mod3, g_post.reshape(1, d))


def _group_layer(x, mod3, lam, wm, wvt, w_out_bf, g_pre, g_post, g_sub, tables, bias, lam_init):
    s = x.shape[1]
    assert s % PROJ_TM == 0
    qa, ka, ga, qb, kb, gb, vat, vbt = _inproj(x, mod3, g_pre, wm, wvt, tables, tm=PROJ_TM)
    ya = _attention(qa, ka, vat, ga, banded=True, tq=A_TQ, tk=A_TK, extra=(bias,))
    yb = _attention(qb, kb, vbt, gb, banded=False, tq=B_TQ, tk=B_TK,
                    extra=(g_sub.reshape(LANES, 1), lam), post_scale=1.0 - lam_init)
    return _outproj(ya, yb, w_out_bf, x, mod3, g_post, tm=PROJ_TM)


def kernel(x_prompt, x_sample, c_prompt, c_sample, w_in, w_out, g_pre, g_post, w_ada, b_ada,
           lam_q1, lam_k1, lam_q2, lam_k2, g_sub):
    depth = w_in.shape[0]
    nbp = x_prompt.shape[0]
    tables = _rope_tables(max(x_prompt.shape[1], x_sample.shape[1]))
    bias = _band_bias(A_TQ, A_TK)
    y_prompt, y_sample = x_prompt, x_sample
    for l in range(depth):
        lam_init = 0.8 - 0.6 * math.exp(-0.3 * l)
        wl = w_in[l]
        seg = lambda n: wl[:, n * SEG:(n + 1) * SEG]
        wm = jnp.concatenate([seg(0), seg(1), seg(3), seg(4), seg(5), seg(7)], axis=1).astype(BF16)
        wvt = jnp.concatenate([seg(2), seg(6)], axis=1).T.astype(BF16)
        w_out_bf = w_out[l].astype(BF16)
        c_all = jnp.concatenate([c_prompt, c_sample], axis=0)
        mod, lam_tile = _modulation(c_all, w_ada[l], b_ada[l], lam_q1[l], lam_k1[l],
                                    lam_q2[l], lam_k2[l], lam_init)
        lam = lam_tile[:1, :1]
        mod3 = mod[:, None, :]
        args = (wm, wvt, w_out_bf, g_pre[l], g_post[l], g_sub[l], tables, bias, lam_init)
        y_prompt = _group_layer(y_prompt, mod3[:nbp], lam, *args)
        y_sample = _group_layer(y_sample, mod3[nbp:], lam, *args)
    return (y_prompt, y_sample)
```

```python
import functools
import math

import numpy as np
import jax
import jax.numpy as jnp
from jax import lax
from jax.experimental import pallas as pl
from jax.experimental.pallas import tpu as pltpu

F32 = jnp.float32
BF16 = jnp.bfloat16

D_MODEL = 1024
HEAD_DIM = 64
SEG = 512
LANES = 128
ROPE_DIM = HEAD_DIM // 4
ROPE_THETA = 500000.0
NORM_EPS = 1e-6
MASK_VALUE = -1e30
DILATED_PATTERNS = ((128, 1), (512, 4), (2048, 16))
A_REACH = max(w // 2 for w, _ in DILATED_PATTERNS)
A_TQ, A_TK = 512, 256
B_TQ, B_TK_MAX = 256, 1024
PROJ_TM = 512
VMEM_LIMIT = 56 * 1024 * 1024

NT_DIMS = (((1,), (1,)), ((), ()))


def _silu(x):
    return x / (1.0 + jnp.exp(-x))


def _ada_kernel(c_ref, w_ref, b_ref, q1_ref, k1_ref, q2_ref, k2_ref, mod_ref, lam_ref, *, lam_init):
    a = _silu(c_ref[...])
    mod_ref[...] = jnp.dot(a, w_ref[...], precision=lax.Precision.HIGHEST,
                           preferred_element_type=F32) + b_ref[...]
    lam = (jnp.exp(jnp.sum(q1_ref[...] * k1_ref[...], keepdims=True))
           - jnp.exp(jnp.sum(q2_ref[...] * k2_ref[...], keepdims=True)) + lam_init)
    lam_ref[...] = jnp.broadcast_to(lam, lam_ref.shape)


def _modulation(c_all, w_ada, b_ada, lq1, lk1, lq2, lk2, lam_init):
    nb = c_all.shape[0]
    small = pl.BlockSpec((1, HEAD_DIM), lambda j: (0, 0))
    return pl.pallas_call(
        functools.partial(_ada_kernel, lam_init=lam_init),
        grid=(3,),
        in_specs=[pl.BlockSpec((nb, D_MODEL), lambda j: (0, 0)),
                  pl.BlockSpec((D_MODEL, D_MODEL), lambda j: (0, j)),
                  pl.BlockSpec((1, D_MODEL), lambda j: (0, j)),
                  small, small, small, small],
        out_specs=[pl.BlockSpec((nb, D_MODEL), lambda j: (0, j)),
                   pl.BlockSpec((8, LANES), lambda j: (0, 0))],
        out_shape=[jax.ShapeDtypeStruct((nb, 3 * D_MODEL), F32),
                   jax.ShapeDtypeStruct((8, LANES), F32)],
        compiler_params=pltpu.CompilerParams(dimension_semantics=("arbitrary",),
                                             vmem_limit_bytes=VMEM_LIMIT),
        name="ada_modulation",
    )(c_all, w_ada, b_ada.reshape(1, -1), lq1.reshape(1, -1), lk1.reshape(1, -1),
      lq2.reshape(1, -1), lk2.reshape(1, -1))


def _rope_tables(seq):
    half = ROPE_DIM // 2
    inv = ROPE_THETA ** (-jnp.arange(half, dtype=F32) / half)
    ang = jnp.arange(seq).astype(F32)[:, None] * inv[None, :]
    cos, sin = jnp.cos(ang), jnp.sin(ang)
    ones = jnp.ones((seq, HEAD_DIM - ROPE_DIM), F32)
    zeros = jnp.zeros((seq, HEAD_DIM - ROPE_DIM), F32)
    zh = jnp.zeros((seq, half), F32)
    c = jnp.concatenate([cos, cos, ones], axis=1)
    sa = jnp.concatenate([-sin, zh, zeros], axis=1)
    sb = jnp.concatenate([zh, sin, zeros], axis=1)
    rep = LANES // HEAD_DIM
    return jnp.tile(c, (1, rep)), jnp.tile(sa, (1, rep)), jnp.tile(sb, (1, rep))


def _inproj_kernel(x_ref, mod_ref, gpre_ref, wm_ref, wvt_ref, cos_ref, sa_ref, sb_ref,
                   qa_ref, ka_ref, ga_ref, qb_ref, kb_ref, gb_ref, vat_ref, vbt_ref):
    x = x_ref[0]
    var = jnp.mean(x * x, axis=-1, keepdims=True)
    xn = x * lax.rsqrt(var + NORM_EPS) * gpre_ref[...]
    mod = mod_ref[0]
    shift = mod[:, :D_MODEL]
    scale = mod[:, D_MODEL:2 * D_MODEL]
    h = (xn * (1.0 + scale) + shift).astype(BF16)

    cos, sa, sb = cos_ref[...], sa_ref[...], sb_ref[...]
    half = ROPE_DIM // 2

    def rope(seg, mult):
        cols = []
        for c in range(SEG // LANES):
            xc = seg[:, c * LANES:(c + 1) * LANES]
            r = (xc * cos + pltpu.roll(xc, LANES - half, axis=1) * sa
                 + pltpu.roll(xc, half, axis=1) * sb)
            cols.append(r * mult if mult != 1.0 else r)
        return jnp.concatenate(cols, axis=1)

    qk_scale = HEAD_DIM ** -0.5 * math.log2(math.e)
    plan = ((qa_ref, True, qk_scale), (ka_ref, True, 1.0), (ga_ref, False, 1.0),
            (qb_ref, True, qk_scale), (kb_ref, True, 1.0), (gb_ref, False, 1.0))
    for s, (ref, roped, mult) in enumerate(plan):
        seg = jnp.dot(h, wm_ref[:, s * SEG:(s + 1) * SEG], preferred_element_type=F32)
        if roped:
            seg = rope(seg, mult)
        ref[0] = seg.astype(ref.dtype)

    vt = lax.dot_general(wvt_ref[...], h, NT_DIMS, preferred_element_type=F32)
    vat_ref[0] = vt[:SEG].astype(vat_ref.dtype)
    vbt_ref[0] = vt[SEG:].astype(vbt_ref.dtype)


def _inproj(x, mod3, g_pre, wm, wvt, tables, tm):
    bn, s, d = x.shape
    cos, sa, sb = tables
    tok = lambda: pl.BlockSpec((1, tm, SEG), lambda b, i: (b, i, 0))
    tokt = lambda: pl.BlockSpec((1, SEG, tm), lambda b, i: (b, 0, i))
    tab = lambda: pl.BlockSpec((tm, LANES), lambda b, i: (i, 0))
    nat = jax.ShapeDtypeStruct((bn, s, SEG), BF16)
    tr = jax.ShapeDtypeStruct((bn, SEG, s), BF16)
    return pl.pallas_call(
        _inproj_kernel,
        grid=(bn, s // tm),
        in_specs=[pl.BlockSpec((1, tm, d), lambda b, i: (b, i, 0)),
                  pl.BlockSpec((1, 1, 3 * d), lambda b, i: (b, 0, 0)),
                  pl.BlockSpec((1, d), lambda b, i: (0, 0)),
                  pl.BlockSpec(wm.shape, lambda b, i: (0, 0)),
                  pl.BlockSpec(wvt.shape, lambda b, i: (0, 0)),
                  tab(), tab(), tab()],
        out_specs=[tok(), tok(), tok(), tok(), tok(), tok(), tokt(), tokt()],
        out_shape=[nat, nat, nat, nat, nat, nat, tr, tr],
        compiler_params=pltpu.CompilerParams(dimension_semantics=("arbitrary", "arbitrary"),
                                             vmem_limit_bytes=VMEM_LIMIT),
        name="inproj_rope",
    )(x, mod3, g_pre.reshape(1, d), wm, wvt, cos[:s], sa[:s], sb[:s])


def _band_bias(tq, tk):
    nd = A_REACH // tk
    delta = np.arange(-nd, nd + tq // tk)[:, None, None]
    key = np.arange(tk)[None, :, None]
    qry = np.arange(tq)[None, None, :]
    rel = delta * tk + key - qry
    mult = np.zeros(rel.shape, np.int64)
    for window, dil in DILATED_PATTERNS:
        mult += ((rel % dil == 0) & (np.abs(rel) <= window // 2)).astype(np.int64)
    bias = np.where(mult > 0, np.log2(np.maximum(mult, 1)), MASK_VALUE)
    return jnp.asarray(bias, F32)


def _attn_kernel(*refs, banded, tq, tk, nkb, post_scale):
    if banded:
        q_ref, k_ref, vt_ref, g_ref, bias_ref, o_ref = refs[:6]
    else:
        q_ref, k_ref, vt_ref, g_ref, gsub_ref, lam_ref, o_ref = refs[:7]
    q2_scr, s_scr, cm_scr, p_scr, acc_scr = refs[-5:]
    i = pl.program_id(2)

    q = q_ref[0].astype(F32)
    lane = lax.broadcasted_iota(jnp.int32, q.shape, 1)
    q2_scr[0] = jnp.where(lane < HEAD_DIM, q, 0.0).astype(BF16)
    q2_scr[1] = jnp.where(lane >= HEAD_DIM, q, 0.0).astype(BF16)
    acc_scr[...] = jnp.zeros(acc_scr.shape, F32)

    if banded:
        ratio, nd = tq // tk, A_REACH // tk
        lo = jnp.maximum(i * ratio - nd, 0)
        npairs = (jnp.minimum(i * ratio + ratio + nd, nkb) - lo) // 2
    else:
        lo = 0

    def qk_stage(j, slot):
        off = pl.multiple_of(j * tk, tk)
        kk = k_ref[0, pl.ds(off, tk), :]
        for mp in range(2):
            s = lax.dot_general(kk, q2_scr[mp], NT_DIMS, preferred_element_type=F32)
            if banded:
                s = s + bias_ref[j - i * ratio + nd]
            s_scr[slot, mp] = s
            cm_scr[slot, mp] = jnp.max(s, axis=0, keepdims=True)

    def sm_stage(slot, m, l):
        m_out, l_out, alpha_out = [], [], []
        for mp in range(2):
            m_new = jnp.maximum(m[mp], cm_scr[slot, mp])
            alpha = jnp.exp2(m[mp] - m_new)
            p = jnp.exp2(s_scr[slot, mp] - m_new)
            l_out.append(alpha * l[mp] + jnp.sum(p, axis=0, keepdims=True))
            p_scr[slot, mp] = p.astype(BF16)
            m_out.append(m_new)
            alpha_out.append(alpha)
        return tuple(m_out), tuple(l_out), tuple(alpha_out)

    def pv_stage(j, slot, alpha):
        off = pl.multiple_of(j * tk, tk)
        vt = vt_ref[0, :, pl.ds(off, tk)]
        for mp in range(2):
            v = vt[mp * HEAD_DIM:(mp + 1) * HEAD_DIM] if banded else vt
            acc_scr[mp] = alpha[mp] * acc_scr[mp] + jnp.dot(v, p_scr[slot, mp],
                                                            preferred_element_type=F32)

    def run(n_pairs):
        m = (jnp.full((1, tq), MASK_VALUE, F32),) * 2
        l = (jnp.zeros((1, tq), F32),) * 2
        qk_stage(lo, 0)
        qk_stage(lo + 1, 1)
        m, l, alpha0 = sm_stage(0, m, l)
        for u in range(n_pairs - 1):
            r = lo + 2 * u
            qk_stage(r + 2, 0)
            m, l, alpha1 = sm_stage(1, m, l)
            pv_stage(r, 0, alpha0)
            qk_stage(r + 3, 1)
            m, l, alpha0 = sm_stage(0, m, l)
            pv_stage(r + 1, 1, alpha1)
        r = lo + 2 * (n_pairs - 1)
        m, l, alpha1 = sm_stage(1, m, l)
        pv_stage(r, 0, alpha0)
        pv_stage(r + 1, 1, alpha1)

        if banded:
            o_t = jnp.concatenate([acc_scr[0] / l[0], acc_scr[1] / l[1]], axis=0)
        else:
            d = acc_scr[0] / l[0] - lam_ref[0, 0] * (acc_scr[1] / l[1])
            var = jnp.mean(d * d, axis=0, keepdims=True)
            o_t = d * lax.rsqrt(var + NORM_EPS) * gsub_ref[...] * post_scale
        g = g_ref[0].astype(F32)
        o_ref[0] = (o_t.T * _silu(g)).astype(o_ref.dtype)

    if banded:
        nq = nkb * tk // tq
        counts = sorted({(min(b * ratio + ratio + nd, nkb) - max(b * ratio - nd, 0)) // 2
                         for b in range(nq)})
        for c in counts:
            pl.when(npairs == c)(functools.partial(run, c))
    else:
        run(nkb // 2)


def _attention(q, k, vt, g, *, banded, tq, tk, extra, post_scale=1.0):
    bn, s, _ = q.shape
    nh = SEG // LANES
    nkb = s // tk
    assert s % tq == 0 and s % tk == 0 and nkb % 2 == 0
    if banded:
        assert tq % (2 * tk) == 0 and A_REACH % (2 * tk) == 0
    kernel = functools.partial(_attn_kernel, banded=banded, tq=tq, tk=tk, nkb=nkb,
                               post_scale=post_scale)
    qspec = lambda: pl.BlockSpec((1, tq, LANES), lambda b, h, i: (b, i, h))
    in_specs = [qspec(),
                pl.BlockSpec((1, s, LANES), lambda b, h, i: (b, 0, h)),
                pl.BlockSpec((1, LANES, s), lambda b, h, i: (b, h, 0)),
                qspec()]
    if banded:
        (bias,) = extra
        in_specs.append(pl.BlockSpec(bias.shape, lambda b, h, i: (0, 0, 0)))
    else:
        in_specs.append(pl.BlockSpec((LANES, 1), lambda b, h, i: (0, 0)))
        in_specs.append(pl.BlockSpec(memory_space=pltpu.SMEM))
    vrows = HEAD_DIM if banded else LANES
    return pl.pallas_call(
        kernel,
        grid=(bn, nh, s // tq),
        in_specs=in_specs,
        out_specs=qspec(),
        out_shape=jax.ShapeDtypeStruct((bn, s, SEG), BF16),
        scratch_shapes=[pltpu.VMEM((2, tq, LANES), BF16),
                        pltpu.VMEM((2, 2, tk, tq), F32),
                        pltpu.VMEM((2, 2, 1, tq), F32),
                        pltpu.VMEM((2, 2, tk, tq), BF16),
                        pltpu.VMEM((2, vrows, tq), F32)],
        compiler_params=pltpu.CompilerParams(
            dimension_semantics=("arbitrary", "arbitrary", "arbitrary"),
            vmem_limit_bytes=VMEM_LIMIT),
        name="mixer_a_banded" if banded else "mixer_b_diff",
    )(q, k, vt, g, *extra)


def _outproj_kernel(ya_ref, yb_ref, w_ref, x_ref, mod_ref, gpost_ref, o_ref):
    y = (jnp.dot(ya_ref[0], w_ref[:SEG, :], preferred_element_type=F32)
         + jnp.dot(yb_ref[0], w_ref[SEG:, :], preferred_element_type=F32))
    var = jnp.mean(y * y, axis=-1, keepdims=True)
    yn = y * lax.rsqrt(var + NORM_EPS) * gpost_ref[...]
    gate = mod_ref[0][:, 2 * D_MODEL:]
    o_ref[0] = x_ref[0] + gate * yn


def _outproj(ya, yb, w_out, x, mod3, g_post, tm):
    bn, s, d = x.shape
    tok = lambda: pl.BlockSpec((1, tm, SEG), lambda b, i: (b, i, 0))
    full = lambda: pl.BlockSpec((1, tm, d), lambda b, i: (b, i, 0))
    return pl.pallas_call(
        _outproj_kernel,
        grid=(bn, s // tm),
        in_specs=[tok(), tok(),
                  pl.BlockSpec(w_out.shape, lambda b, i: (0, 0)),
                  full(),
                  pl.BlockSpec((1, 1, 3 * d), lambda b, i: (b, 0, 0)),
                  pl.BlockSpec((1, d), lambda b, i: (0, 0))],
        out_specs=full(),
        out_shape=jax.ShapeDtypeStruct((bn, s, d), F32),
        compiler_params=pltpu.CompilerParams(dimension_semantics=("arbitrary", "arbitrary"),
                                             vmem_limit_bytes=VMEM_LIMIT),
        name="outproj_residual",
    )(ya, yb, w_out, x, mod3, g_post.reshape(1, d))


def _group_layer(x, mod3, lam, wm, wvt, w_out_bf, g_pre, g_post, g_sub, tables, bias, lam_init):
    s = x.shape[1]
    assert s % PROJ_TM == 0
    qa, ka, ga, qb, kb, gb, vat, vbt = _inproj(x, mod3, g_pre, wm, wvt, tables, tm=PROJ_TM)
    ya = _attention(qa, ka, vat, ga, banded=True, tq=A_TQ, tk=A_TK, extra=(bias,))
    yb = _attention(qb, kb, vbt, gb, banded=False, tq=B_TQ, tk=min(B_TK_MAX, s // 4),
                    extra=(g_sub.reshape(LANES, 1), lam), post_scale=1.0 - lam_init)
    return _outproj(ya, yb, w_out_bf, x, mod3, g_post, tm=PROJ_TM)


def kernel(x_prompt, x_sample, c_prompt, c_sample, w_in, w_out, g_pre, g_post, w_ada, b_ada,
           lam_q1, lam_k1, lam_q2, lam_k2, g_sub):
    depth = w_in.shape[0]
    nbp = x_prompt.shape[0]
    tables = _rope_tables(max(x_prompt.shape[1], x_sample.shape[1]))
    bias = _band_bias(A_TQ, A_TK)
    y_prompt, y_sample = x_prompt, x_sample
    for l in range(depth):
        lam_init = 0.8 - 0.6 * math.exp(-0.3 * l)
        wl = w_in[l]
        seg = lambda n: wl[:, n * SEG:(n + 1) * SEG]
        wm = jnp.concatenate([seg(0), seg(1), seg(3), seg(4), seg(5), seg(7)], axis=1).astype(BF16)
        wvt = jnp.concatenate([seg(2), seg(6)], axis=1).T.astype(BF16)
        w_out_bf = w_out[l].astype(BF16)
        c_all = jnp.concatenate([c_prompt, c_sample], axis=0)
        mod, lam_tile = _modulation(c_all, w_ada[l], b_ada[l], lam_q1[l], lam_k1[l],
                                    lam_q2[l], lam_k2[l], lam_init)
        lam = lam_tile[:1, :1]
        mod3 = mod[:, None, :]
        args = (wm, wvt, w_out_bf, g_pre[l], g_post[l], g_sub[l], tables, bias, lam_init)
        y_prompt = _group_layer(y_prompt, mod3[:nbp], lam, *args)
        y_sample = _group_layer(y_sample, mod3[nbp:], lam, *args)
    return (y_prompt, y_sample)
```
